```python
import math
import jax
import jax.numpy as jnp
from jax import lax
import numpy as np

D_MODEL = 1024
BATCH = 4
SEQ = 4096
DEPTH = 1

D_MIX = D_MODEL
D_CONV = D_MIX // 2
D_SSM = D_MIX - D_CONV
CONV_WIDTH = 31
SSM_GROUP = 16
SSM_GROUPS = D_SSM // SSM_GROUP
SSM_STATE = 64
OUT_HEAD_DIM = 64
D_IN = 2 * D_CONV + D_SSM
D_FF = ((8 * D_MODEL // 3 + 127) // 128) * 128
FFN_RES = 0.5
EPS = 1e-6
DT_MIN = 1e-3
DT_MAX = 1e-1

kernel_name = 'hybrid_conformer_s5_encoder_layer'


def rms_norm(x, g):
    xf = x.astype(jnp.float32)
    y = xf * lax.rsqrt(jnp.mean(xf * xf, axis=-1, keepdims=True) + EPS)
    return (y * g.astype(jnp.float32)).astype(x.dtype)


def head_rms_norm(x, g, head_dim):
    b, l, d = x.shape
    xh = x.astype(jnp.float32).reshape(b, l, d // head_dim, head_dim)
    y = xh * lax.rsqrt(jnp.mean(xh * xh, axis=-1, keepdims=True) + EPS)
    return (y.reshape(b, l, d) * g.astype(jnp.float32)).astype(x.dtype)


def layer_norm(x, g, b):
    xf = x.astype(jnp.float32)
    mu = jnp.mean(xf, axis=-1, keepdims=True)
    var = jnp.mean(jnp.square(xf - mu), axis=-1, keepdims=True)
    y = (xf - mu) * lax.rsqrt(var + EPS)
    return (y * g.astype(jnp.float32) + b.astype(jnp.float32)).astype(x.dtype)


def swiglu(u, w_gate, w_up, w_down):
    return (jax.nn.silu(u @ w_gate) * (u @ w_up)) @ w_down


def depthwise_conv(v, w, b):
    rhs = w[:, None, :]
    pad = CONV_WIDTH // 2
    y = lax.conv_general_dilated(
        v, rhs, window_strides=(1,), padding=[(pad, pad)],
        dimension_numbers=('NWC', 'WIO', 'NWC'),
        feature_group_count=v.shape[-1])
    return y + b


def _linear_recurrence(e1, e2):
    a1, b1 = e1
    a2, b2 = e2
    return a1 * a2, a2 * b1 + b2


def s5_direction(u, lam_re, lam_im, log_step, b_re, b_im, c_re, c_im, reverse):
    lam = lax.complex(lam_re.astype(jnp.float32), lam_im.astype(jnp.float32))
    step = jnp.exp(log_step.astype(jnp.float32))[:, None]
    a_bar = jnp.exp(lam * step)
    b_c = lax.complex(b_re.astype(jnp.float32), b_im.astype(jnp.float32))
    b_bar = ((a_bar - 1.0) / lam)[..., None] * b_c
    bu = jnp.einsum('blgh,gph->blgp', u.astype(jnp.complex64), b_bar)
    a = jnp.broadcast_to(a_bar, bu.shape)
    _, states = lax.associative_scan(_linear_recurrence, (a, bu), axis=1, reverse=reverse)
    c_c = lax.complex(c_re.astype(jnp.float32), c_im.astype(jnp.float32))
    return jnp.real(jnp.einsum('blgp,ghp->blgh', states, c_c))


def hybrid_mixer(u, w_in, b_in, conv_w, conv_b, conv_ln_g, conv_ln_b, conv_out_g,
                 lam_re_f, lam_im_f, log_step_f, b_re_f, b_im_f, c_re_f, c_im_f,
                 lam_re_b, lam_im_b, log_step_b, b_re_b, b_im_b, c_re_b, c_im_b,
                 ssm_d, ssm_glu_w, ssm_glu_b, ssm_out_g, w_out, b_out):
    bsz, seq, _ = u.shape
    z = u @ w_in + b_in
    conv_val = z[..., :D_CONV]
    conv_gate = z[..., D_CONV:2 * D_CONV]
    ssm_in = z[..., 2 * D_CONV:]

    g = conv_val * jax.nn.sigmoid(conv_gate)
    d = depthwise_conv(g, conv_w, conv_b)
    d = jax.nn.silu(layer_norm(d, conv_ln_g, conv_ln_b))
    conv_y = head_rms_norm(d, conv_out_g, OUT_HEAD_DIM)

    s = ssm_in.astype(jnp.float32).reshape(bsz, seq, SSM_GROUPS, SSM_GROUP)
    y = (s5_direction(s, lam_re_f, lam_im_f, log_step_f, b_re_f, b_im_f, c_re_f, c_im_f, False)
         + s5_direction(s, lam_re_b, lam_im_b, log_step_b, b_re_b, b_im_b, c_re_b, c_im_b, True)
         + s * ssm_d.astype(jnp.float32).reshape(SSM_GROUPS, SSM_GROUP))
    y = jax.nn.gelu(y.reshape(bsz, seq, D_SSM).astype(u.dtype))
    y = y * jax.nn.sigmoid(y @ ssm_glu_w + ssm_glu_b)
    ssm_y = head_rms_norm(y, ssm_out_g, OUT_HEAD_DIM)

    return jnp.concatenate([conv_y, ssm_y], axis=-1) @ w_out + b_out


def setup_inputs(seed: int = 0) -> dict:
    key = jax.random.key(seed)
    ks = iter(jax.random.split(key, 64))
    f32 = jnp.float32

    def normal(shape, scale):
        return jax.random.normal(next(ks), shape, f32) * scale

    def gain(shape):
        return 1.0 + normal(shape, 0.05)

    def ssm_dir():
        lam_re = -0.5 + normal((DEPTH, SSM_GROUPS, SSM_STATE), 0.01)
        lam_im = jnp.pi * jnp.arange(SSM_STATE, dtype=f32) + normal((DEPTH, SSM_GROUPS, SSM_STATE), 0.01)
        log_step = jax.random.uniform(next(ks), (DEPTH, SSM_GROUPS), f32,
                                      minval=math.log(DT_MIN), maxval=math.log(DT_MAX))
        b_re = normal((DEPTH, SSM_GROUPS, SSM_STATE, SSM_GROUP), (2 * SSM_GROUP) ** -0.5)
        b_im = normal((DEPTH, SSM_GROUPS, SSM_STATE, SSM_GROUP), (2 * SSM_GROUP) ** -0.5)
        c_re = normal((DEPTH, SSM_GROUPS, SSM_GROUP, SSM_STATE), SSM_STATE ** -0.5)
        c_im = normal((DEPTH, SSM_GROUPS, SSM_GROUP, SSM_STATE), SSM_STATE ** -0.5)
        return lam_re, lam_im, log_step, b_re, b_im, c_re, c_im

    x = normal((BATCH, SEQ, D_MODEL), 1.0)
    inp = {}
    inp['x'] = x
    inp['ffn1_pre_g'] = gain((DEPTH, D_MODEL))
    inp['ffn1_w_gate'] = normal((DEPTH, D_MODEL, D_FF), D_MODEL ** -0.5)
    inp['ffn1_w_up'] = normal((DEPTH, D_MODEL, D_FF), D_MODEL ** -0.5)
    inp['ffn1_w_down'] = normal((DEPTH, D_FF, D_MODEL), D_FF ** -0.5)
    inp['ffn1_post_g'] = gain((DEPTH, D_MODEL))
    inp['mix_pre_g'] = gain((DEPTH, D_MODEL))
    inp['w_in'] = normal((DEPTH, D_MODEL, D_IN), D_MODEL ** -0.5)
    inp['b_in'] = normal((DEPTH, D_IN), 0.02)
    inp['conv_w'] = normal((DEPTH, CONV_WIDTH, D_CONV), CONV_WIDTH ** -0.5)
    inp['conv_b'] = normal((DEPTH, D_CONV), 0.02)
    inp['conv_ln_g'] = gain((DEPTH, D_CONV))
    inp['conv_ln_b'] = normal((DEPTH, D_CONV), 0.02)
    inp['conv_out_g'] = gain((DEPTH, D_CONV))
    (inp['lam_re_f'], inp['lam_im_f'], inp['log_step_f'], inp['b_re_f'],
     inp['b_im_f'], inp['c_re_f'], inp['c_im_f']) = ssm_dir()
    (inp['lam_re_b'], inp['lam_im_b'], inp['log_step_b'], inp['b_re_b'],
     inp['b_im_b'], inp['c_re_b'], inp['c_im_b']) = ssm_dir()
    inp['ssm_d'] = normal((DEPTH, D_SSM), 1.0)
    inp['ssm_glu_w'] = normal((DEPTH, D_SSM, D_SSM), D_SSM ** -0.5)
    inp['ssm_glu_b'] = normal((DEPTH, D_SSM), 0.02)
    inp['ssm_out_g'] = gain((DEPTH, D_SSM))
    inp['w_out'] = normal((DEPTH, D_MIX, D_MODEL), D_MIX ** -0.5)
    inp['b_out'] = normal((DEPTH, D_MODEL), 0.02)
    inp['mix_post_g'] = gain((DEPTH, D_MODEL))
    inp['ffn2_pre_g'] = gain((DEPTH, D_MODEL))
    inp['ffn2_w_gate'] = normal((DEPTH, D_MODEL, D_FF), D_MODEL ** -0.5)
    inp['ffn2_w_up'] = normal((DEPTH, D_MODEL, D_FF), D_MODEL ** -0.5)
    inp['ffn2_w_down'] = normal((DEPTH, D_FF, D_MODEL), D_FF ** -0.5)
    inp['ffn2_post_g'] = gain((DEPTH, D_MODEL))
    return inp


def reference(x, ffn1_pre_g, ffn1_w_gate, ffn1_w_up, ffn1_w_down, ffn1_post_g,
              mix_pre_g, w_in, b_in, conv_w, conv_b, conv_ln_g, conv_ln_b, conv_out_g,
              lam_re_f, lam_im_f, log_step_f, b_re_f, b_im_f, c_re_f, c_im_f,
              lam_re_b, lam_im_b, log_step_b, b_re_b, b_im_b, c_re_b, c_im_b,
              ssm_d, ssm_glu_w, ssm_glu_b, ssm_out_g, w_out, b_out, mix_post_g,
              ffn2_pre_g, ffn2_w_gate, ffn2_w_up, ffn2_w_down, ffn2_post_g):
    h = x
    for l in range(DEPTH):
        f = swiglu(rms_norm(h, ffn1_pre_g[l]), ffn1_w_gate[l], ffn1_w_up[l], ffn1_w_down[l])
        h = h + FFN_RES * rms_norm(f, ffn1_post_g[l])
        m = hybrid_mixer(rms_norm(h, mix_pre_g[l]), w_in[l], b_in[l],
                         conv_w[l], conv_b[l], conv_ln_g[l], conv_ln_b[l], conv_out_g[l],
                         lam_re_f[l], lam_im_f[l], log_step_f[l], b_re_f[l], b_im_f[l], c_re_f[l], c_im_f[l],
                         lam_re_b[l], lam_im_b[l], log_step_b[l], b_re_b[l], b_im_b[l], c_re_b[l], c_im_b[l],
                         ssm_d[l], ssm_glu_w[l], ssm_glu_b[l], ssm_out_g[l], w_out[l], b_out[l])
        h = h + rms_norm(m, mix_post_g[l])
        f = swiglu(rms_norm(h, ffn2_pre_g[l]), ffn2_w_gate[l], ffn2_w_up[l], ffn2_w_down[l])
        h = h + FFN_RES * rms_norm(f, ffn2_post_g[l])
    return h
```

```python
import functools

import jax
import jax.numpy as jnp
from jax import lax
from jax.experimental import pallas as pl
from jax.experimental.pallas import tpu as pltpu

D_MODEL = 1024
D_CONV = 512
D_SSM = 512
D_IN = 2 * D_CONV + D_SSM
CONV_WIDTH = 31
CONV_PAD = CONV_WIDTH // 2
SSM_GROUP = 16
SSM_GROUPS = D_SSM // SSM_GROUP
SSM_STATE = 64
OUT_HEAD_DIM = 64
D_FF = 2816
FFN_RES = 0.5
EPS = 1e-6

T_CHUNK = 16
CHUNK_K = T_CHUNK * SSM_GROUP
STATE_W = 4 * SSM_STATE
SUBLANES = 8
HALO = 16

ROW_TILE = 256
S5_GROUP_BLOCK = 4
CONV_ROW_BLOCK = 32
VMEM_LIMIT_BYTES = 56 * 1024 * 1024

F32 = jnp.float32
BF16 = jnp.bfloat16


def _rms_norm(x, g):
    return x * lax.rsqrt(jnp.mean(x * x, axis=-1, keepdims=True) + EPS) * g


def _swiglu_half_step(h, pre_g, w_gate, w_up, w_down, post_g):
    xn = _rms_norm(h, pre_g).astype(BF16)
    gate = jnp.dot(xn, w_gate, preferred_element_type=F32)
    up = jnp.dot(xn, w_up, preferred_element_type=F32)
    hid = (gate * jax.nn.sigmoid(gate) * up).astype(BF16)
    f = jnp.dot(hid, w_down, preferred_element_type=F32)
    return h + FFN_RES * _rms_norm(f, post_g)


def _ffn_in_proj_kernel(x_ref, pre_g_ref, wg_ref, wu_ref, wd_ref, post_g_ref,
                        mix_g_ref, w_in_ref, b_in_ref,
                        h_ref, g_ref, s_ref):
    h = _swiglu_half_step(x_ref[...], pre_g_ref[...], wg_ref[...], wu_ref[...],
                          wd_ref[...], post_g_ref[...])
    h_ref[...] = h
    u = _rms_norm(h, mix_g_ref[...]).astype(BF16)
    z = jnp.dot(u, w_in_ref[...], preferred_element_type=F32) + b_in_ref[...]
    g_ref[...] = z[:, :D_CONV] * jax.nn.sigmoid(z[:, D_CONV:2 * D_CONV])
    s_ref[...] = z[:, 2 * D_CONV:]


def _resident(shape):
    zeros = (0,) * len(shape)
    return pl.BlockSpec(shape, lambda *_: zeros, pipeline_mode=pl.Buffered(1))


def _rows(width):
    return pl.BlockSpec((ROW_TILE, width), lambda i: (i, 0))


def _ffn_in_proj(x2, pre_g, wg, wu, wd, post_g, mix_g, w_in, b_in):
    n = x2.shape[0]
    return pl.pallas_call(
        _ffn_in_proj_kernel,
        grid=(n // ROW_TILE,),
        in_specs=[_rows(D_MODEL), _resident((1, D_MODEL)),
                  _resident((D_MODEL, D_FF)), _resident((D_MODEL, D_FF)),
                  _resident((D_FF, D_MODEL)), _resident((1, D_MODEL)),
                  _resident((1, D_MODEL)), _resident((D_MODEL, D_IN)),
                  _resident((1, D_IN))],
        out_specs=[_rows(D_MODEL), _rows(D_CONV), _rows(D_SSM)],
        out_shape=[jax.ShapeDtypeStruct((n, D_MODEL), F32),
                   jax.ShapeDtypeStruct((n, D_CONV), F32),
                   jax.ShapeDtypeStruct((n, D_SSM), F32)],
        compiler_params=pltpu.CompilerParams(
            dimension_semantics=("arbitrary",), vmem_limit_bytes=VMEM_LIMIT_BYTES),
        name="ffn1_in_proj",
    )(x2, pre_g, wg, wu, wd, post_g, mix_g, w_in, b_in)


def _s5_kernel(n_chunks, n_batch, u_ref, m_ref, w_ref, v_ref, a_ref,
               y_ref, s_scr, pf_scr, pb_scr):
    gb = u_ref.shape[0]
    half = 2 * SSM_STATE
    pair = 2 * n_batch
    n_pairs = n_chunks // 2
    for j in range(gb):
        s_scr[j] = jnp.dot(u_ref[j], w_ref[j], preferred_element_type=F32)

    is_fwd = lax.broadcasted_iota(jnp.int32, (pair, half), 1) < SSM_STATE
    top = lax.broadcasted_iota(jnp.int32, (pair, half), 0) < n_batch
    a_re = [jnp.broadcast_to(a_ref[j, 0:1, :], (pair, half)) for j in range(gb)]
    a_im = [jnp.broadcast_to(a_ref[j, 1:2, :], (pair, half)) for j in range(gb)]
    swap = lambda v: pltpu.roll(v, n_batch, 0)

    def step(i, carry):
        rf = pl.multiple_of(i * pair, pair)
        rb = pl.multiple_of((n_pairs - 1 - i) * pair, pair)
        new = []
        for j in range(gb):
            xr, xi = carry[j]
            sr = jnp.where(is_fwd, s_scr[j, pl.ds(rf, pair), 0:half],
                           swap(s_scr[j, pl.ds(rb, pair), 0:half]))
            si = jnp.where(is_fwd, s_scr[j, pl.ds(rf, pair), half:2 * half],
                           swap(s_scr[j, pl.ds(rb, pair), half:2 * half]))
            ur = a_re[j] * xr - a_im[j] * xi + sr
            ui = a_re[j] * xi + a_im[j] * xr + si
            ur = jnp.where(top, ur, swap(ur))
            ui = jnp.where(top, ui, swap(ui))
            vr = a_re[j] * ur - a_im[j] * ui + sr
            vi = a_re[j] * ui + a_im[j] * ur + si
            pf_scr[j, pl.ds(rf, pair), 0:half] = jnp.where(top, xr, ur)
            pf_scr[j, pl.ds(rf, pair), half:2 * half] = jnp.where(top, xi, ui)
            pb_scr[j, pl.ds(rb, pair), 0:half] = jnp.where(top, ur, xr)
            pb_scr[j, pl.ds(rb, pair), half:2 * half] = jnp.where(top, ui, xi)
            new.append((jnp.where(top, swap(vr), vr), jnp.where(top, swap(vi), vi)))
        return tuple(new)

    zero = jnp.zeros((pair, half), F32)
    lax.fori_loop(0, n_pairs, step, tuple((zero, zero) for _ in range(gb)))

    rows = n_chunks * n_batch
    fwd_lane = (lax.broadcasted_iota(jnp.int32, (rows, STATE_W), 1) % half) < SSM_STATE
    for j in range(gb):
        carried = jnp.where(fwd_lane, pf_scr[j], pb_scr[j]).astype(BF16)
        y_ref[j] = (jnp.dot(u_ref[j], m_ref[j], preferred_element_type=F32)
                    + jnp.dot(carried, v_ref[j], preferred_element_type=F32))


def _s5(u, m_op, w_op, v_op, a_op, n_chunks, n_batch):
    groups, rows, _ = u.shape
    gb = S5_GROUP_BLOCK
    assert 2 * n_batch == SUBLANES and n_chunks % 2 == 0
    blk = lambda *shape: pl.BlockSpec((gb,) + shape, lambda i: (i, 0, 0))
    return pl.pallas_call(
        functools.partial(_s5_kernel, n_chunks, n_batch),
        grid=(groups // gb,),
        in_specs=[blk(rows, CHUNK_K), blk(CHUNK_K, CHUNK_K), blk(CHUNK_K, STATE_W),
                  blk(STATE_W, CHUNK_K), blk(2, 2 * SSM_STATE)],
        out_specs=blk(rows, CHUNK_K),
        out_shape=jax.ShapeDtypeStruct((groups, rows, CHUNK_K), F32),
        scratch_shapes=[pltpu.VMEM((gb, rows, STATE_W), F32)] * 3,
        compiler_params=pltpu.CompilerParams(
            dimension_semantics=("arbitrary",), vmem_limit_bytes=VMEM_LIMIT_BYTES),
        name="s5_chunked",
    )(u, m_op, w_op, v_op, a_op)


def _s5_direction_terms(lam_re, lam_im, log_step, b_re, b_im, c_re, c_im):
    dt = jnp.exp(log_step)[:, None]
    x = lam_re * dt
    th = lam_im * dt
    k = jnp.arange(T_CHUNK + 1, dtype=F32)[:, None, None]
    mag = jnp.exp(k * x[None])
    pw_re = mag * jnp.cos(k * th[None])
    pw_im = mag * jnp.sin(k * th[None])
    num_re = jnp.expm1(x) * jnp.cos(th) - 2.0 * jnp.square(jnp.sin(0.5 * th))
    num_im = jnp.exp(x) * jnp.sin(th)
    den = lam_re * lam_re + lam_im * lam_im
    q_re = (num_re * lam_re + num_im * lam_im) / den
    q_im = (num_im * lam_re - num_re * lam_im) / den
    bb_re = q_re[..., None] * b_re - q_im[..., None] * b_im
    bb_im = q_re[..., None] * b_im + q_im[..., None] * b_re
    return pw_re, pw_im, bb_re, bb_im


def _s5_operators(fwd, bwd, ssm_d):
    hi = lax.Precision.HIGHEST
    t = T_CHUNK
    fr, fi, fbr, fbi = _s5_direction_terms(*fwd)
    br, bi, bbr, bbi = _s5_direction_terms(*bwd)
    c_re_f, c_im_f = fwd[5], fwd[6]
    c_re_b, c_im_b = bwd[5], bwd[6]

    def lag_kernels(pw_re, pw_im, bb_re, bb_im, c_re, c_im):
        cp_re = c_re[None] * pw_re[:t, :, None, :] - c_im[None] * pw_im[:t, :, None, :]
        cp_im = c_re[None] * pw_im[:t, :, None, :] + c_im[None] * pw_re[:t, :, None, :]
        return (jnp.einsum('kgop,gph->kgoh', cp_re, bb_re, precision=hi)
                - jnp.einsum('kgop,gph->kgoh', cp_im, bb_im, precision=hi))

    kf = lag_kernels(fr, fi, fbr, fbi, c_re_f, c_im_f)
    kb = lag_kernels(br, bi, bbr, bbi, c_re_b, c_im_b)
    tin = jnp.arange(t)[:, None]
    tout = jnp.arange(t)[None, :]
    lag = tout - tin
    mf = jnp.where((lag >= 0)[..., None, None, None], kf[jnp.clip(lag, 0, t - 1)], 0.0)
    mb = jnp.where((lag <= 0)[..., None, None, None], kb[jnp.clip(-lag, 0, t - 1)], 0.0)
    m = jnp.transpose(mf + mb, (2, 0, 4, 1, 3))
    skip = (jnp.eye(t, dtype=F32)[None, :, None, :, None]
            * jnp.eye(SSM_GROUP, dtype=F32)[None, None, :, None, :]
            * ssm_d.reshape(SSM_GROUPS, 1, SSM_GROUP, 1, 1))
    m_op = (m + skip).reshape(SSM_GROUPS, CHUNK_K, CHUNK_K)

    def summary_weights(pw_re, pw_im, bb_re, bb_im, exps):
        pr = jnp.transpose(pw_re[exps], (1, 0, 2))[:, :, None, :]
        pi = jnp.transpose(pw_im[exps], (1, 0, 2))[:, :, None, :]
        b_r = jnp.transpose(bb_re, (0, 2, 1))[:, None]
        b_i = jnp.transpose(bb_im, (0, 2, 1))[:, None]
        return pr * b_r - pi * b_i, pr * b_i + pi * b_r

    wf_re, wf_im = summary_weights(fr, fi, fbr, fbi, jnp.arange(t - 1, -1, -1))
    wb_re, wb_im = summary_weights(br, bi, bbr, bbi, jnp.arange(t))
    w_op = jnp.concatenate([wf_re, wb_re, wf_im, wb_im], axis=-1).reshape(
        SSM_GROUPS, CHUNK_K, STATE_W)

    def readout_weights(pw_re, pw_im, c_re, c_im, exps):
        pr = jnp.transpose(pw_re[exps], (1, 2, 0))[..., None]
        pi = jnp.transpose(pw_im[exps], (1, 2, 0))[..., None]
        cr = jnp.transpose(c_re, (0, 2, 1))[:, :, None, :]
        ci = jnp.transpose(c_im, (0, 2, 1))[:, :, None, :]
        return cr * pr - ci * pi, -(cr * pi + ci * pr)

    vf_re, vf_im = readout_weights(fr, fi, c_re_f, c_im_f, jnp.arange(1, t + 1))
    vb_re, vb_im = readout_weights(br, bi, c_re_b, c_im_b, jnp.arange(t, 0, -1))
    v_op = jnp.concatenate([vf_re, vb_re, vf_im, vb_im], axis=1).reshape(
        SSM_GROUPS, STATE_W, CHUNK_K)

    a_op = jnp.stack([jnp.concatenate([fr[t], br[t]], axis=-1),
                      jnp.concatenate([fi[t], bi[t]], axis=-1)], axis=1)
    return m_op.astype(BF16), w_op.astype(BF16), v_op.astype(BF16), a_op


def _head_rms_norm(x, head_avg, g):
    ms = jnp.dot((x * x).astype(BF16), head_avg, preferred_element_type=F32)
    return x * lax.rsqrt(ms + EPS) * g


def _mixer_ffn_kernel(tiles_per_seq,
                      h_ref, gprev_ref, g_ref, gnext_ref, y_ref,
                      conv_w_ref, conv_b_ref, ln_g_ref, ln_b_ref, conv_out_g_ref,
                      glu_w_ref, glu_b_ref, ssm_out_g_ref, head_avg_ref,
                      w_out_ref, b_out_ref, mix_post_g_ref,
                      pre_g_ref, wg_ref, wu_ref, wd_ref, post_g_ref,
                      out_ref, gbuf, dbuf):
    pos = pl.program_id(0) % tiles_per_seq
    gbuf[0:HALO, :] = jnp.where(pos == 0, 0.0, gprev_ref[...])
    gbuf[HALO:HALO + ROW_TILE, :] = g_ref[...]
    gbuf[HALO + ROW_TILE:, :] = jnp.where(pos == tiles_per_seq - 1, 0.0, gnext_ref[...])

    for r in range(0, ROW_TILE, CONV_ROW_BLOCK):
        acc = jnp.broadcast_to(conv_b_ref[...], (CONV_ROW_BLOCK, D_CONV))
        for k in range(CONV_WIDTH):
            start = r + HALO - CONV_PAD + k
            acc = acc + conv_w_ref[k:k + 1, :] * gbuf[start:start + CONV_ROW_BLOCK, :]
        dbuf[r:r + CONV_ROW_BLOCK, :] = acc

    d = dbuf[...]
    mu = jnp.mean(d, axis=-1, keepdims=True)
    dc = d - mu
    var = jnp.mean(dc * dc, axis=-1, keepdims=True)
    d = dc * lax.rsqrt(var + EPS) * ln_g_ref[...] + ln_b_ref[...]
    d = d * jax.nn.sigmoid(d)
    head_avg = head_avg_ref[...]
    conv_y = _head_rms_norm(d, head_avg, conv_out_g_ref[...])

    y = jax.nn.gelu(y_ref[...])
    gate = jnp.dot(y.astype(BF16), glu_w_ref[...], preferred_element_type=F32) + glu_b_ref[...]
    y = y * jax.nn.sigmoid(gate)
    ssm_y = _head_rms_norm(y, head_avg, ssm_out_g_ref[...])

    cat = jnp.concatenate([conv_y, ssm_y], axis=-1).astype(BF16)
    m = jnp.dot(cat, w_out_ref[...], preferred_element_type=F32) + b_out_ref[...]
    h = h_ref[...] + _rms_norm(m, mix_post_g_ref[...])
    out_ref[...] = _swiglu_half_step(h, pre_g_ref[...], wg_ref[...], wu_ref[...],
                                     wd_ref[...], post_g_ref[...])


def _mixer_ffn(h1, g, y, seq_len, conv_w, conv_b, ln_g, ln_b, conv_out_g,
               glu_w, glu_b, ssm_out_g, head_avg, w_out, b_out, mix_post_g,
               pre_g, wg, wu, wd, post_g):
    n = h1.shape[0]
    tiles_per_seq = seq_len // ROW_TILE
    halo_per_tile = ROW_TILE // HALO
    n_halo = n // HALO
    prev_spec = pl.BlockSpec(
        (HALO, D_CONV), lambda i: (jnp.maximum(i * halo_per_tile - 1, 0), 0))
    next_spec = pl.BlockSpec(
        (HALO, D_CONV), lambda i: (jnp.minimum((i + 1) * halo_per_tile, n_halo - 1), 0))
    return pl.pallas_call(
        functools.partial(_mixer_ffn_kernel, tiles_per_seq),
        grid=(n // ROW_TILE,),
        in_specs=[_rows(D_MODEL), prev_spec, _rows(D_CONV), next_spec, _rows(D_SSM),
                  _resident((CONV_WIDTH, D_CONV)), _resident((1, D_CONV)),
                  _resident((1, D_CONV)), _resident((1, D_CONV)), _resident((1, D_CONV)),
                  _resident((D_SSM, D_SSM)), _resident((1, D_SSM)), _resident((1, D_SSM)),
                  _resident((D_CONV, D_CONV)),
                  _resident((D_MODEL, D_MODEL)), _resident((1, D_MODEL)),
                  _resident((1, D_MODEL)),
                  _resident((1, D_MODEL)), _resident((D_MODEL, D_FF)),
                  _resident((D_MODEL, D_FF)), _resident((D_FF, D_MODEL)),
                  _resident((1, D_MODEL))],
        out_specs=_rows(D_MODEL),
        out_shape=jax.ShapeDtypeStruct((n, D_MODEL), F32),
        scratch_shapes=[pltpu.VMEM((ROW_TILE + 2 * HALO, D_CONV), F32),
                        pltpu.VMEM((ROW_TILE, D_CONV), F32)],
        compiler_params=pltpu.CompilerParams(
            dimension_semantics=("arbitrary",), vmem_limit_bytes=VMEM_LIMIT_BYTES),
        name="mixer_ffn2",
    )(h1, g, g, g, y, conv_w, conv_b, ln_g, ln_b, conv_out_g, glu_w, glu_b, ssm_out_g,
      head_avg, w_out, b_out, mix_post_g, pre_g, wg, wu, wd, post_g)


def kernel(x, ffn1_pre_g, ffn1_w_gate, ffn1_w_up, ffn1_w_down, ffn1_post_g, mix_pre_g, w_in, b_in, conv_w, conv_b, conv_ln_g, conv_ln_b, conv_out_g, lam_re_f, lam_im_f, log_step_f, b_re_f, b_im_f, c_re_f, c_im_f, lam_re_b, lam_im_b, log_step_b, b_re_b, b_im_b, c_re_b, c_im_b, ssm_d, ssm_glu_w, ssm_glu_b, ssm_out_g, w_out, b_out, mix_post_g, ffn2_pre_g, ffn2_w_gate, ffn2_w_up, ffn2_w_down, ffn2_post_g):
    bsz, seq, _ = x.shape
    n = bsz * seq
    n_chunks = seq // T_CHUNK
    row = lambda v: v.reshape(1, -1)
    h = x.reshape(n, D_MODEL)
    for l in range(ffn1_pre_g.shape[0]):
        h1, g, s = _ffn_in_proj(
            h, row(ffn1_pre_g[l]), ffn1_w_gate[l].astype(BF16), ffn1_w_up[l].astype(BF16),
            ffn1_w_down[l].astype(BF16), row(ffn1_post_g[l]), row(mix_pre_g[l]),
            w_in[l].astype(BF16), row(b_in[l]))

        m_op, w_op, v_op, a_op = _s5_operators(
            (lam_re_f[l], lam_im_f[l], log_step_f[l], b_re_f[l], b_im_f[l], c_re_f[l], c_im_f[l]),
            (lam_re_b[l], lam_im_b[l], log_step_b[l], b_re_b[l], b_im_b[l], c_re_b[l], c_im_b[l]),
            ssm_d[l])
        u = s.reshape(bsz, n_chunks, T_CHUNK, SSM_GROUPS, SSM_GROUP)
        u = jnp.transpose(u, (3, 1, 0, 2, 4)).reshape(SSM_GROUPS, n_chunks * bsz, CHUNK_K)
        yk = _s5(u.astype(BF16), m_op, w_op, v_op, a_op, n_chunks, bsz)
        yk = yk.reshape(SSM_GROUPS, n_chunks, bsz, T_CHUNK, SSM_GROUP)
        y = jnp.transpose(yk, (2, 1, 3, 0, 4)).reshape(n, D_SSM)

        head_avg = jnp.kron(jnp.eye(D_CONV // OUT_HEAD_DIM, dtype=F32),
                            jnp.full((OUT_HEAD_DIM, OUT_HEAD_DIM), 1.0 / OUT_HEAD_DIM, F32))
        h = _mixer_ffn(
            h1, g, y, seq, conv_w[l], row(conv_b[l]), row(conv_ln_g[l]), row(conv_ln_b[l]),
            row(conv_out_g[l]), ssm_glu_w[l].astype(BF16), row(ssm_glu_b[l]),
            row(ssm_out_g[l]), head_avg.astype(BF16), w_out[l].astype(BF16), row(b_out[l]),
            row(mix_post_g[l]), row(ffn2_pre_g[l]), ffn2_w_gate[l].astype(BF16),
            ffn2_w_up[l].astype(BF16), ffn2_w_down[l].astype(BF16), row(ffn2_post_g[l]))
    return h.reshape(bsz, seq, D_MODEL)
```

```python
import functools

import jax
import jax.numpy as jnp
import numpy as np
from jax import lax
from jax.experimental import pallas as pl
from jax.experimental.pallas import tpu as pltpu

D_MODEL = 1024
D_CONV = 512
D_SSM = 512
D_IN = 2 * D_CONV + D_SSM
CONV_WIDTH = 31
CONV_PAD = CONV_WIDTH // 2
SSM_GROUP = 16
SSM_GROUPS = D_SSM // SSM_GROUP
SSM_STATE = 64
OUT_HEAD_DIM = 64
D_FF = 2816
FFN_RES = 0.5
EPS = 1e-6

T_CHUNK = 16
CHUNK_K = T_CHUNK * SSM_GROUP
STATE_W = 4 * SSM_STATE
SUBLANES = 8
LANES = 128
LANE_TILES = D_SSM // LANES
PIECES = LANES // SSM_GROUP
PERM_W = PIECES * LANES
RELAYOUT_ROWS = 2048
HALO = 16

ROW_TILE = 256
S5_GROUP_BLOCK = 4
CONV_ROW_BLOCK = 32
VMEM_LIMIT_BYTES = 56 * 1024 * 1024

F32 = jnp.float32
BF16 = jnp.bfloat16


def _rms_norm(x, g):
    return x * lax.rsqrt(jnp.mean(x * x, axis=-1, keepdims=True) + EPS) * g


def _swiglu_half_step(h, pre_g, w_gate, w_up, w_down, post_g):
    xn = _rms_norm(h, pre_g).astype(BF16)
    gate = jnp.dot(xn, w_gate, preferred_element_type=F32)
    up = jnp.dot(xn, w_up, preferred_element_type=F32)
    hid = (gate * jax.nn.sigmoid(gate) * up).astype(BF16)
    f = jnp.dot(hid, w_down, preferred_element_type=F32)
    return h + FFN_RES * _rms_norm(f, post_g)


def _ffn_in_proj_kernel(x_ref, pre_g_ref, wg_ref, wu_ref, wd_ref, post_g_ref,
                        mix_g_ref, w_in_ref, b_in_ref,
                        h_ref, g_ref, s_ref):
    h = _swiglu_half_step(x_ref[...], pre_g_ref[...], wg_ref[...], wu_ref[...],
                          wd_ref[...], post_g_ref[...])
    h_ref[...] = h
    u = _rms_norm(h, mix_g_ref[...]).astype(BF16)
    z = jnp.dot(u, w_in_ref[...], preferred_element_type=F32) + b_in_ref[...]
    g_ref[...] = z[:, :D_CONV] * jax.nn.sigmoid(z[:, D_CONV:2 * D_CONV])
    s_ref[...] = z[:, 2 * D_CONV:]


def _resident(shape):
    zeros = (0,) * len(shape)
    return pl.BlockSpec(shape, lambda *_: zeros, pipeline_mode=pl.Buffered(1))


def _rows(width):
    return pl.BlockSpec((ROW_TILE, width), lambda i: (i, 0))


def _ffn_in_proj(x2, pre_g, wg, wu, wd, post_g, mix_g, w_in, b_in):
    n = x2.shape[0]
    return pl.pallas_call(
        _ffn_in_proj_kernel,
        grid=(n // ROW_TILE,),
        in_specs=[_rows(D_MODEL), _resident((1, D_MODEL)),
                  _resident((D_MODEL, D_FF)), _resident((D_MODEL, D_FF)),
                  _resident((D_FF, D_MODEL)), _resident((1, D_MODEL)),
                  _resident((1, D_MODEL)), _resident((D_MODEL, D_IN)),
                  _resident((1, D_IN))],
        out_specs=[_rows(D_MODEL), _rows(D_CONV), _rows(D_SSM)],
        out_shape=[jax.ShapeDtypeStruct((n, D_MODEL), F32),
                   jax.ShapeDtypeStruct((n, D_CONV), F32),
                   jax.ShapeDtypeStruct((n, D_SSM), F32)],
        compiler_params=pltpu.CompilerParams(
            dimension_semantics=("arbitrary",), vmem_limit_bytes=VMEM_LIMIT_BYTES),
        name="ffn1_in_proj",
    )(x2, pre_g, wg, wu, wd, post_g, mix_g, w_in, b_in)


def _piece_transpose_matrix():
    idx = np.arange(PERM_W)
    a, b, h = idx // LANES, (idx // SSM_GROUP) % PIECES, idx % SSM_GROUP
    return (b * LANES + a * SSM_GROUP + h)[:, None] == idx[None, :]


def _to_group_major_kernel(z_ref, perm_ref, u_ref, zq_scr, cat_scr):
    c = z_ref.shape[0] // T_CHUNK
    for q in range(LANE_TILES):
        zq_scr[q] = z_ref[:, q * LANES:(q + 1) * LANES]
    for q in range(LANE_TILES):
        for r in range(CHUNK_K // LANES):
            base = (q * (CHUNK_K // LANES) + r) * c
            for s in range(PIECES):
                cat_scr[base:base + c, s * LANES:(s + 1) * LANES] = zq_scr[
                    q, pl.ds(PIECES * r + s, c, stride=T_CHUNK), :].astype(BF16)
    out = jnp.dot(cat_scr[...], perm_ref[...], preferred_element_type=F32)
    for q in range(LANE_TILES):
        for r in range(CHUNK_K // LANES):
            base = (q * (CHUNK_K // LANES) + r) * c
            for j in range(PIECES):
                u_ref[q * PIECES + j, :, r * LANES:(r + 1) * LANES] = out[
                    base:base + c, j * LANES:(j + 1) * LANES].astype(BF16)


def _to_group_major(z, perm):
    n = z.shape[0]
    c = RELAYOUT_ROWS // T_CHUNK
    return pl.pallas_call(
        _to_group_major_kernel,
        grid=(n // RELAYOUT_ROWS,),
        in_specs=[pl.BlockSpec((RELAYOUT_ROWS, D_SSM), lambda i: (i, 0)),
                  _resident((PERM_W, PERM_W))],
        out_specs=pl.BlockSpec((SSM_GROUPS, c, CHUNK_K), lambda i: (0, i, 0)),
        out_shape=jax.ShapeDtypeStruct((SSM_GROUPS, n // T_CHUNK, CHUNK_K), BF16),
        scratch_shapes=[pltpu.VMEM((LANE_TILES, RELAYOUT_ROWS, LANES), F32),
                        pltpu.VMEM((PIECES * c, PERM_W), BF16)],
        compiler_params=pltpu.CompilerParams(
            dimension_semantics=("arbitrary",), vmem_limit_bytes=VMEM_LIMIT_BYTES),
        name="to_group_major",
    )(z, perm)


def _from_group_major_kernel(y_ref, perm_ref, o_ref, yq_scr, cat_scr):
    c = y_ref.shape[1]
    for q in range(LANE_TILES):
        for r in range(CHUNK_K // LANES):
            base = (q * (CHUNK_K // LANES) + r) * c
            for j in range(PIECES):
                cat_scr[base:base + c, j * LANES:(j + 1) * LANES] = y_ref[
                    q * PIECES + j, :, r * LANES:(r + 1) * LANES].astype(BF16)
    out = jnp.dot(cat_scr[...], perm_ref[...], preferred_element_type=F32)
    for q in range(LANE_TILES):
        for r in range(CHUNK_K // LANES):
            base = (q * (CHUNK_K // LANES) + r) * c
            for s in range(PIECES):
                yq_scr[q, pl.ds(PIECES * r + s, c, stride=T_CHUNK), :] = out[
                    base:base + c, s * LANES:(s + 1) * LANES]
    for q in range(LANE_TILES):
        o_ref[:, q * LANES:(q + 1) * LANES] = yq_scr[q]


def _from_group_major(y, perm):
    n = y.shape[1] * T_CHUNK
    c = RELAYOUT_ROWS // T_CHUNK
    return pl.pallas_call(
        _from_group_major_kernel,
        grid=(n // RELAYOUT_ROWS,),
        in_specs=[pl.BlockSpec((SSM_GROUPS, c, CHUNK_K), lambda i: (0, i, 0)),
                  _resident((PERM_W, PERM_W))],
        out_specs=pl.BlockSpec((RELAYOUT_ROWS, D_SSM), lambda i: (i, 0)),
        out_shape=jax.ShapeDtypeStruct((n, D_SSM), F32),
        scratch_shapes=[pltpu.VMEM((LANE_TILES, RELAYOUT_ROWS, LANES), F32),
                        pltpu.VMEM((PIECES * c, PERM_W), BF16)],
        compiler_params=pltpu.CompilerParams(
            dimension_semantics=("arbitrary",), vmem_limit_bytes=VMEM_LIMIT_BYTES),
        name="from_group_major",
    )(y, perm)


def _cmul(ar, ai, br, bi):
    return ar * br - ai * bi, ar * bi + ai * br


def _s5_kernel(n_chunks, n_batch, u_ref, m_ref, w_ref, v_ref, a_ref,
               y_ref, s_scr, pf_scr, pb_scr):
    gb = u_ref.shape[0]
    half = 2 * SSM_STATE
    blk = SUBLANES
    n_blk = n_chunks // blk
    for j in range(gb):
        s_scr[j] = jnp.dot(u_ref[j], w_ref[j], preferred_element_type=F32)

    row = lax.broadcasted_iota(jnp.int32, (blk, half), 0)
    is_fwd = lax.broadcasted_iota(jnp.int32, (blk, half), 1) < SSM_STATE
    is_fwd_row = lax.broadcasted_iota(jnp.int32, (1, half), 1) < SSM_STATE
    by_dir = lambda f, b: (is_fwd & f) | (~is_fwd & b)
    edge = by_dir(row == 0, row == blk - 1)
    down = lambda v, s: jnp.where(is_fwd, pltpu.roll(v, s, 0), pltpu.roll(v, blk - s, 0))

    for j in range(gb):
        pows = [(a_ref[j, 0:1, :], a_ref[j, 1:2, :])]
        for _ in range(blk - 1):
            pows.append(_cmul(*pows[-1], *pows[0]))
        zero = jnp.zeros((blk, half), F32)
        pw_re, pw_im = zero, zero
        for r in range(blk):
            pw_re = jnp.where(row == r, jnp.where(is_fwd_row, pows[r][0], pows[blk - 1 - r][0]), pw_re)
            pw_im = jnp.where(row == r, jnp.where(is_fwd_row, pows[r][1], pows[blk - 1 - r][1]), pw_im)
        levels = []
        s = 1
        while s < blk:
            valid = by_dir(row >= s, row < blk - s)
            levels.append((s, jnp.where(valid, pows[s - 1][0], 0.0), jnp.where(valid, pows[s - 1][1], 0.0)))
            s *= 2

        def step(i, carry):
            new = []
            for b in range(n_batch):
                cr, ci = carry[b]
                rf = pl.multiple_of(b * n_chunks + i * blk, blk)
                rb = pl.multiple_of(b * n_chunks + (n_blk - 1 - i) * blk, blk)
                xr = jnp.where(is_fwd, s_scr[j, pl.ds(rf, blk), 0:half],
                               s_scr[j, pl.ds(rb, blk), 0:half])
                xi = jnp.where(is_fwd, s_scr[j, pl.ds(rf, blk), half:2 * half],
                               s_scr[j, pl.ds(rb, blk), half:2 * half])
                for s, m_re, m_im in levels:
                    dr, di = _cmul(m_re, m_im, down(xr, s), down(xi, s))
                    xr, xi = xr + dr, xi + di
                dr, di = _cmul(pw_re, pw_im, cr, ci)
                xr, xi = xr + dr, xi + di
                pr = jnp.where(edge, cr, down(xr, 1))
                pi = jnp.where(edge, ci, down(xi, 1))
                pf_scr[j, pl.ds(rf, blk), 0:half] = pr
                pf_scr[j, pl.ds(rf, blk), half:2 * half] = pi
                pb_scr[j, pl.ds(rb, blk), 0:half] = pr
                pb_scr[j, pl.ds(rb, blk), half:2 * half] = pi
                bcast = lambda v: jnp.where(is_fwd, jnp.broadcast_to(v[blk - 1:blk], (blk, half)),
                                            jnp.broadcast_to(v[0:1], (blk, half)))
                new.append((bcast(xr), bcast(xi)))
            return tuple(new)

        lax.fori_loop(0, n_blk, step, tuple((zero, zero) for _ in range(n_batch)))

    rows = n_chunks * n_batch
    fwd_lane = (lax.broadcasted_iota(jnp.int32, (rows, STATE_W), 1) % half) < SSM_STATE
    for j in range(gb):
        carried = jnp.where(fwd_lane, pf_scr[j], pb_scr[j]).astype(BF16)
        y_ref[j] = (jnp.dot(u_ref[j], m_ref[j], preferred_element_type=F32)
                    + jnp.dot(carried, v_ref[j], preferred_element_type=F32))


def _s5(u, m_op, w_op, v_op, a_op, n_chunks, n_batch):
    groups, rows, _ = u.shape
    gb = S5_GROUP_BLOCK
    assert n_chunks % SUBLANES == 0 and rows == n_chunks * n_batch
    blk = lambda *shape: pl.BlockSpec((gb,) + shape, lambda i: (i, 0, 0))
    return pl.pallas_call(
        functools.partial(_s5_kernel, n_chunks, n_batch),
        grid=(groups // gb,),
        in_specs=[blk(rows, CHUNK_K), blk(CHUNK_K, CHUNK_K), blk(CHUNK_K, STATE_W),
                  blk(STATE_W, CHUNK_K), blk(2, 2 * SSM_STATE)],
        out_specs=blk(rows, CHUNK_K),
        out_shape=jax.ShapeDtypeStruct((groups, rows, CHUNK_K), F32),
        scratch_shapes=[pltpu.VMEM((gb, rows, STATE_W), F32)] * 3,
        compiler_params=pltpu.CompilerParams(
            dimension_semantics=("arbitrary",), vmem_limit_bytes=VMEM_LIMIT_BYTES),
        name="s5_chunked",
    )(u, m_op, w_op, v_op, a_op)


def _s5_direction_terms(lam_re, lam_im, log_step, b_re, b_im, c_re, c_im):
    dt = jnp.exp(log_step)[:, None]
    x = lam_re * dt
    th = lam_im * dt
    k = jnp.arange(T_CHUNK + 1, dtype=F32)[:, None, None]
    mag = jnp.exp(k * x[None])
    pw_re = mag * jnp.cos(k * th[None])
    pw_im = mag * jnp.sin(k * th[None])
    num_re = jnp.expm1(x) * jnp.cos(th) - 2.0 * jnp.square(jnp.sin(0.5 * th))
    num_im = jnp.exp(x) * jnp.sin(th)
    den = lam_re * lam_re + lam_im * lam_im
    q_re = (num_re * lam_re + num_im * lam_im) / den
    q_im = (num_im * lam_re - num_re * lam_im) / den
    bb_re = q_re[..., None] * b_re - q_im[..., None] * b_im
    bb_im = q_re[..., None] * b_im + q_im[..., None] * b_re
    return pw_re, pw_im, bb_re, bb_im


def _s5_operators(fwd, bwd, ssm_d):
    hi = lax.Precision.HIGHEST
    t = T_CHUNK
    fr, fi, fbr, fbi = _s5_direction_terms(*fwd)
    br, bi, bbr, bbi = _s5_direction_terms(*bwd)
    c_re_f, c_im_f = fwd[5], fwd[6]
    c_re_b, c_im_b = bwd[5], bwd[6]

    def lag_kernels(pw_re, pw_im, bb_re, bb_im, c_re, c_im):
        cp_re = c_re[None] * pw_re[:t, :, None, :] - c_im[None] * pw_im[:t, :, None, :]
        cp_im = c_re[None] * pw_im[:t, :, None, :] + c_im[None] * pw_re[:t, :, None, :]
        return (jnp.einsum('kgop,gph->kgoh', cp_re, bb_re, precision=hi)
                - jnp.einsum('kgop,gph->kgoh', cp_im, bb_im, precision=hi))

    kf = lag_kernels(fr, fi, fbr, fbi, c_re_f, c_im_f)
    kb = lag_kernels(br, bi, bbr, bbi, c_re_b, c_im_b)
    tin = jnp.arange(t)[:, None]
    tout = jnp.arange(t)[None, :]
    lag = tout - tin
    mf = jnp.where((lag >= 0)[..., None, None, None], kf[jnp.clip(lag, 0, t - 1)], 0.0)
    mb = jnp.where((lag <= 0)[..., None, None, None], kb[jnp.clip(-lag, 0, t - 1)], 0.0)
    m = jnp.transpose(mf + mb, (2, 0, 4, 1, 3))
    skip = (jnp.eye(t, dtype=F32)[None, :, None, :, None]
            * jnp.eye(SSM_GROUP, dtype=F32)[None, None, :, None, :]
            * ssm_d.reshape(SSM_GROUPS, 1, SSM_GROUP, 1, 1))
    m_op = (m + skip).reshape(SSM_GROUPS, CHUNK_K, CHUNK_K)

    def summary_weights(pw_re, pw_im, bb_re, bb_im, exps):
        pr = jnp.transpose(pw_re[exps], (1, 0, 2))[:, :, None, :]
        pi = jnp.transpose(pw_im[exps], (1, 0, 2))[:, :, None, :]
        b_r = jnp.transpose(bb_re, (0, 2, 1))[:, None]
        b_i = jnp.transpose(bb_im, (0, 2, 1))[:, None]
        return pr * b_r - pi * b_i, pr * b_i + pi * b_r

    wf_re, wf_im = summary_weights(fr, fi, fbr, fbi, jnp.arange(t - 1, -1, -1))
    wb_re, wb_im = summary_weights(br, bi, bbr, bbi, jnp.arange(t))
    w_op = jnp.concatenate([wf_re, wb_re, wf_im, wb_im], axis=-1).reshape(
        SSM_GROUPS, CHUNK_K, STATE_W)

    def readout_weights(pw_re, pw_im, c_re, c_im, exps):
        pr = jnp.transpose(pw_re[exps], (1, 2, 0))[..., None]
        pi = jnp.transpose(pw_im[exps], (1, 2, 0))[..., None]
        cr = jnp.transpose(c_re, (0, 2, 1))[:, :, None, :]
        ci = jnp.transpose(c_im, (0, 2, 1))[:, :, None, :]
        return cr * pr - ci * pi, -(cr * pi + ci * pr)

    vf_re, vf_im = readout_weights(fr, fi, c_re_f, c_im_f, jnp.arange(1, t + 1))
    vb_re, vb_im = readout_weights(br, bi, c_re_b, c_im_b, jnp.arange(t, 0, -1))
    v_op = jnp.concatenate([vf_re, vb_re, vf_im, vb_im], axis=1).reshape(
        SSM_GROUPS, STATE_W, CHUNK_K)

    a_op = jnp.stack([jnp.concatenate([fr[t], br[t]], axis=-1),
                      jnp.concatenate([fi[t], bi[t]], axis=-1)], axis=1)
    return m_op.astype(BF16), w_op.astype(BF16), v_op.astype(BF16), a_op


def _head_rms_norm(x, head_avg, g):
    ms = jnp.dot((x * x).astype(BF16), head_avg, preferred_element_type=F32)
    return x * lax.rsqrt(ms + EPS) * g


def _mixer_ffn_kernel(tiles_per_seq,
                      h_ref, gprev_ref, g_ref, gnext_ref, y_ref,
                      conv_w_ref, conv_b_ref, ln_g_ref, ln_b_ref, conv_out_g_ref,
                      glu_w_ref, glu_b_ref, ssm_out_g_ref, head_avg_ref,
                      w_out_ref, b_out_ref, mix_post_g_ref,
                      pre_g_ref, wg_ref, wu_ref, wd_ref, post_g_ref,
                      out_ref, gbuf, dbuf):
    pos = pl.program_id(0) % tiles_per_seq
    gbuf[0:HALO, :] = jnp.where(pos == 0, 0.0, gprev_ref[...])
    gbuf[HALO:HALO + ROW_TILE, :] = g_ref[...]
    gbuf[HALO + ROW_TILE:, :] = jnp.where(pos == tiles_per_seq - 1, 0.0, gnext_ref[...])

    for r in range(0, ROW_TILE, CONV_ROW_BLOCK):
        acc = jnp.broadcast_to(conv_b_ref[...], (CONV_ROW_BLOCK, D_CONV))
        for k in range(CONV_WIDTH):
            start = r + HALO - CONV_PAD + k
            acc = acc + conv_w_ref[k:k + 1, :] * gbuf[start:start + CONV_ROW_BLOCK, :]
        dbuf[r:r + CONV_ROW_BLOCK, :] = acc

    d = dbuf[...]
    mu = jnp.mean(d, axis=-1, keepdims=True)
    dc = d - mu
    var = jnp.mean(dc * dc, axis=-1, keepdims=True)
    d = dc * lax.rsqrt(var + EPS) * ln_g_ref[...] + ln_b_ref[...]
    d = d * jax.nn.sigmoid(d)
    head_avg = head_avg_ref[...]
    conv_y = _head_rms_norm(d, head_avg, conv_out_g_ref[...])

    y = jax.nn.gelu(y_ref[...])
    gate = jnp.dot(y.astype(BF16), glu_w_ref[...], preferred_element_type=F32) + glu_b_ref[...]
    y = y * jax.nn.sigmoid(gate)
    ssm_y = _head_rms_norm(y, head_avg, ssm_out_g_ref[...])

    cat = jnp.concatenate([conv_y, ssm_y], axis=-1).astype(BF16)
    m = jnp.dot(cat, w_out_ref[...], preferred_element_type=F32) + b_out_ref[...]
    h = h_ref[...] + _rms_norm(m, mix_post_g_ref[...])
    out_ref[...] = _swiglu_half_step(h, pre_g_ref[...], wg_ref[...], wu_ref[...],
                                     wd_ref[...], post_g_ref[...])


def _mixer_ffn(h1, g, y, seq_len, conv_w, conv_b, ln_g, ln_b, conv_out_g,
               glu_w, glu_b, ssm_out_g, head_avg, w_out, b_out, mix_post_g,
               pre_g, wg, wu, wd, post_g):
    n = h1.shape[0]
    tiles_per_seq = seq_len // ROW_TILE
    halo_per_tile = ROW_TILE // HALO
    n_halo = n // HALO
    prev_spec = pl.BlockSpec(
        (HALO, D_CONV), lambda i: (jnp.maximum(i * halo_per_tile - 1, 0), 0))
    next_spec = pl.BlockSpec(
        (HALO, D_CONV), lambda i: (jnp.minimum((i + 1) * halo_per_tile, n_halo - 1), 0))
    return pl.pallas_call(
        functools.partial(_mixer_ffn_kernel, tiles_per_seq),
        grid=(n // ROW_TILE,),
        in_specs=[_rows(D_MODEL), prev_spec, _rows(D_CONV), next_spec, _rows(D_SSM),
                  _resident((CONV_WIDTH, D_CONV)), _resident((1, D_CONV)),
                  _resident((1, D_CONV)), _resident((1, D_CONV)), _resident((1, D_CONV)),
                  _resident((D_SSM, D_SSM)), _resident((1, D_SSM)), _resident((1, D_SSM)),
                  _resident((D_CONV, D_CONV)),
                  _resident((D_MODEL, D_MODEL)), _resident((1, D_MODEL)),
                  _resident((1, D_MODEL)),
                  _resident((1, D_MODEL)), _resident((D_MODEL, D_FF)),
                  _resident((D_MODEL, D_FF)), _resident((D_FF, D_MODEL)),
                  _resident((1, D_MODEL))],
        out_specs=_rows(D_MODEL),
        out_shape=jax.ShapeDtypeStruct((n, D_MODEL), F32),
        scratch_shapes=[pltpu.VMEM((ROW_TILE + 2 * HALO, D_CONV), F32),
                        pltpu.VMEM((ROW_TILE, D_CONV), F32)],
        compiler_params=pltpu.CompilerParams(
            dimension_semantics=("arbitrary",), vmem_limit_bytes=VMEM_LIMIT_BYTES),
        name="mixer_ffn2",
    )(h1, g, g, g, y, conv_w, conv_b, ln_g, ln_b, conv_out_g, glu_w, glu_b, ssm_out_g,
      head_avg, w_out, b_out, mix_post_g, pre_g, wg, wu, wd, post_g)


def kernel(x, ffn1_pre_g, ffn1_w_gate, ffn1_w_up, ffn1_w_down, ffn1_post_g, mix_pre_g, w_in, b_in, conv_w, conv_b, conv_ln_g, conv_ln_b, conv_out_g, lam_re_f, lam_im_f, log_step_f, b_re_f, b_im_f, c_re_f, c_im_f, lam_re_b, lam_im_b, log_step_b, b_re_b, b_im_b, c_re_b, c_im_b, ssm_d, ssm_glu_w, ssm_glu_b, ssm_out_g, w_out, b_out, mix_post_g, ffn2_pre_g, ffn2_w_gate, ffn2_w_up, ffn2_w_down, ffn2_post_g):
    bsz, seq, _ = x.shape
    n = bsz * seq
    n_chunks = seq // T_CHUNK
    row = lambda v: v.reshape(1, -1)
    h = x.reshape(n, D_MODEL)
    for l in range(ffn1_pre_g.shape[0]):
        h1, g, s = _ffn_in_proj(
            h, row(ffn1_pre_g[l]), ffn1_w_gate[l].astype(BF16), ffn1_w_up[l].astype(BF16),
            ffn1_w_down[l].astype(BF16), row(ffn1_post_g[l]), row(mix_pre_g[l]),
            w_in[l].astype(BF16), row(b_in[l]))

        m_op, w_op, v_op, a_op = _s5_operators(
            (lam_re_f[l], lam_im_f[l], log_step_f[l], b_re_f[l], b_im_f[l], c_re_f[l], c_im_f[l]),
            (lam_re_b[l], lam_im_b[l], log_step_b[l], b_re_b[l], b_im_b[l], c_re_b[l], c_im_b[l]),
            ssm_d[l])
        perm = jnp.asarray(_piece_transpose_matrix(), BF16)
        u = _to_group_major(s, perm)
        yk = _s5(u, m_op, w_op, v_op, a_op, n_chunks, bsz)
        y = _from_group_major(yk, perm)

        head_avg = jnp.kron(jnp.eye(D_CONV // OUT_HEAD_DIM, dtype=F32),
                            jnp.full((OUT_HEAD_DIM, OUT_HEAD_DIM), 1.0 / OUT_HEAD_DIM, F32))
        h = _mixer_ffn(
            h1, g, y, seq, conv_w[l], row(conv_b[l]), row(conv_ln_g[l]), row(conv_ln_b[l]),
            row(conv_out_g[l]), ssm_glu_w[l].astype(BF16), row(ssm_glu_b[l]),
            row(ssm_out_g[l]), head_avg.astype(BF16), w_out[l].astype(BF16), row(b_out[l]),
            row(mix_post_g[l]), row(ffn2_pre_g[l]), ffn2_w_gate[l].astype(BF16),
            ffn2_w_up[l].astype(BF16), ffn2_w_down[l].astype(BF16), row(ffn2_post_g[l]))
    return h.reshape(bsz, seq, D_MODEL)
```

```python
import functools

import jax
import jax.numpy as jnp
import numpy as np
from jax import lax
from jax.experimental import pallas as pl
from jax.experimental.pallas import tpu as pltpu

D_MODEL = 1024
D_CONV = 512
D_SSM = 512
D_IN = 2 * D_CONV + D_SSM
CONV_WIDTH = 31
CONV_PAD = CONV_WIDTH // 2
SSM_GROUP = 16
SSM_GROUPS = D_SSM // SSM_GROUP
SSM_STATE = 64
OUT_HEAD_DIM = 64
D_FF = 2816
FFN_RES = 0.5
EPS = 1e-6

T_CHUNK = 16
CHUNK_K = T_CHUNK * SSM_GROUP
STATE_W = 4 * SSM_STATE
SUBLANES = 8
LANES = 128
LANE_TILES = D_SSM // LANES
PIECES = LANES // SSM_GROUP
PERM_W = PIECES * LANES
RELAYOUT_ROWS = 2048
HALO = 16

ROW_TILE = 256
S5_GROUP_BLOCK = 8
CONV_ROW_BLOCK = 32
VMEM_LIMIT_BYTES = 56 * 1024 * 1024

F32 = jnp.float32
BF16 = jnp.bfloat16
NT_DIMS = (((1,), (1,)), ((), ()))


def _rms_norm(x, g):
    return x * lax.rsqrt(jnp.mean(x * x, axis=-1, keepdims=True) + EPS) * g


def _swiglu_half_step(h, pre_g, w_gate, w_up, w_down, post_g):
    xn = _rms_norm(h, pre_g).astype(BF16)
    gate = jnp.dot(xn, w_gate, preferred_element_type=F32)
    up = jnp.dot(xn, w_up, preferred_element_type=F32)
    hid = (gate * jax.nn.sigmoid(gate) * up).astype(BF16)
    f = jnp.dot(hid, w_down, preferred_element_type=F32)
    return h + FFN_RES * _rms_norm(f, post_g)


def _ffn_in_proj_kernel(x_ref, pre_g_ref, wg_ref, wu_ref, wd_ref, post_g_ref,
                        mix_g_ref, w_in_ref, b_in_ref,
                        h_ref, g_ref, s_ref):
    h = _swiglu_half_step(x_ref[...], pre_g_ref[...], wg_ref[...], wu_ref[...],
                          wd_ref[...], post_g_ref[...])
    h_ref[...] = h
    u = _rms_norm(h, mix_g_ref[...]).astype(BF16)
    z = jnp.dot(u, w_in_ref[...], preferred_element_type=F32) + b_in_ref[...]
    g_ref[...] = z[:, :D_CONV] * jax.nn.sigmoid(z[:, D_CONV:2 * D_CONV])
    s_ref[...] = z[:, 2 * D_CONV:]


def _resident(shape):
    zeros = (0,) * len(shape)
    return pl.BlockSpec(shape, lambda *_: zeros, pipeline_mode=pl.Buffered(1))


def _rows(width):
    return pl.BlockSpec((ROW_TILE, width), lambda i: (i, 0))


def _ffn_in_proj(x2, pre_g, wg, wu, wd, post_g, mix_g, w_in, b_in):
    n = x2.shape[0]
    return pl.pallas_call(
        _ffn_in_proj_kernel,
        grid=(n // ROW_TILE,),
        in_specs=[_rows(D_MODEL), _resident((1, D_MODEL)),
                  _resident((D_MODEL, D_FF)), _resident((D_MODEL, D_FF)),
                  _resident((D_FF, D_MODEL)), _resident((1, D_MODEL)),
                  _resident((1, D_MODEL)), _resident((D_MODEL, D_IN)),
                  _resident((1, D_IN))],
        out_specs=[_rows(D_MODEL), _rows(D_CONV), _rows(D_SSM)],
        out_shape=[jax.ShapeDtypeStruct((n, D_MODEL), F32),
                   jax.ShapeDtypeStruct((n, D_CONV), F32),
                   jax.ShapeDtypeStruct((n, D_SSM), F32)],
        compiler_params=pltpu.CompilerParams(
            dimension_semantics=("arbitrary",), vmem_limit_bytes=VMEM_LIMIT_BYTES),
        name="ffn1_in_proj",
    )(x2, pre_g, wg, wu, wd, post_g, mix_g, w_in, b_in)


def _piece_transpose_matrix():
    idx = np.arange(PERM_W)
    a, b, h = idx // LANES, (idx // SSM_GROUP) % PIECES, idx % SSM_GROUP
    return (b * LANES + a * SSM_GROUP + h)[:, None] == idx[None, :]


def _to_group_major_kernel(z_ref, perm_ref, u_ref, zq_scr, cat_scr):
    c = z_ref.shape[0] // T_CHUNK
    for q in range(LANE_TILES):
        zq_scr[q] = z_ref[:, q * LANES:(q + 1) * LANES]
    for q in range(LANE_TILES):
        for r in range(CHUNK_K // LANES):
            base = (q * (CHUNK_K // LANES) + r) * c
            for s in range(PIECES):
                cat_scr[base:base + c, s * LANES:(s + 1) * LANES] = zq_scr[
                    q, pl.ds(PIECES * r + s, c, stride=T_CHUNK), :].astype(BF16)
    out = jnp.dot(cat_scr[...], perm_ref[...], preferred_element_type=F32)
    for q in range(LANE_TILES):
        for r in range(CHUNK_K // LANES):
            base = (q * (CHUNK_K // LANES) + r) * c
            for j in range(PIECES):
                u_ref[q * PIECES + j, :, r * LANES:(r + 1) * LANES] = out[
                    base:base + c, j * LANES:(j + 1) * LANES].astype(BF16)


def _to_group_major(z, perm):
    n = z.shape[0]
    c = RELAYOUT_ROWS // T_CHUNK
    return pl.pallas_call(
        _to_group_major_kernel,
        grid=(n // RELAYOUT_ROWS,),
        in_specs=[pl.BlockSpec((RELAYOUT_ROWS, D_SSM), lambda i: (i, 0)),
                  _resident((PERM_W, PERM_W))],
        out_specs=pl.BlockSpec((SSM_GROUPS, c, CHUNK_K), lambda i: (0, i, 0)),
        out_shape=jax.ShapeDtypeStruct((SSM_GROUPS, n // T_CHUNK, CHUNK_K), BF16),
        scratch_shapes=[pltpu.VMEM((LANE_TILES, RELAYOUT_ROWS, LANES), F32),
                        pltpu.VMEM((PIECES * c, PERM_W), BF16)],
        compiler_params=pltpu.CompilerParams(
            dimension_semantics=("arbitrary",), vmem_limit_bytes=VMEM_LIMIT_BYTES),
        name="to_group_major",
    )(z, perm)


def _from_group_major_kernel(y_ref, perm_ref, o_ref, yq_scr, cat_scr):
    c = y_ref.shape[1]
    for q in range(LANE_TILES):
        for r in range(CHUNK_K // LANES):
            base = (q * (CHUNK_K // LANES) + r) * c
            for j in range(PIECES):
                cat_scr[base:base + c, j * LANES:(j + 1) * LANES] = y_ref[
                    q * PIECES + j, :, r * LANES:(r + 1) * LANES]
    out = jnp.dot(cat_scr[...], perm_ref[...], preferred_element_type=F32)
    for q in range(LANE_TILES):
        for r in range(CHUNK_K // LANES):
            base = (q * (CHUNK_K // LANES) + r) * c
            for s in range(PIECES):
                yq_scr[q, pl.ds(PIECES * r + s, c, stride=T_CHUNK), :] = out[
                    base:base + c, s * LANES:(s + 1) * LANES]
    for q in range(LANE_TILES):
        o_ref[:, q * LANES:(q + 1) * LANES] = yq_scr[q]


def _from_group_major(y, perm):
    n = y.shape[1] * T_CHUNK
    c = RELAYOUT_ROWS // T_CHUNK
    return pl.pallas_call(
        _from_group_major_kernel,
        grid=(n // RELAYOUT_ROWS,),
        in_specs=[pl.BlockSpec((SSM_GROUPS, c, CHUNK_K), lambda i: (0, i, 0)),
                  _resident((PERM_W, PERM_W))],
        out_specs=pl.BlockSpec((RELAYOUT_ROWS, D_SSM), lambda i: (i, 0)),
        out_shape=jax.ShapeDtypeStruct((n, D_SSM), F32),
        scratch_shapes=[pltpu.VMEM((LANE_TILES, RELAYOUT_ROWS, LANES), F32),
                        pltpu.VMEM((PIECES * c, PERM_W), BF16)],
        compiler_params=pltpu.CompilerParams(
            dimension_semantics=("arbitrary",), vmem_limit_bytes=VMEM_LIMIT_BYTES),
        name="from_group_major",
    )(y, perm)


def _cmul(ar, ai, br, bi):
    return ar * br - ai * bi, ar * bi + ai * br


def _cexp(k, x, th):
    mag = jnp.exp(k * x)
    return mag * jnp.cos(k * th), mag * jnp.sin(k * th)


def _repeat_rows(v):
    return jnp.concatenate(
        [jnp.broadcast_to(v[t:t + 1], (SSM_GROUP, v.shape[1])) for t in range(v.shape[0])], axis=0)


def _tile_rows(v):
    return jnp.concatenate([v] * T_CHUNK, axis=0)


def _s5_group_operators(lam_re, lam_im, dt, b_re, b_im, c_re, c_im, d_row):
    half = 2 * SSM_STATE
    t = T_CHUNK
    x = lam_re * dt
    th = lam_im * dt
    ex = jnp.exp(x)
    em1 = jnp.where(ex == 1.0, x, (ex - 1.0) * x / jnp.log(jnp.where(ex == 1.0, 2.0, ex)))
    num_re = em1 * jnp.cos(th) - 2.0 * jnp.square(jnp.sin(0.5 * th))
    num_im = ex * jnp.sin(th)
    den = lam_re * lam_re + lam_im * lam_im
    q_re = (num_re * lam_re + num_im * lam_im) / den
    q_im = (num_im * lam_re - num_re * lam_im) / den

    step = lax.broadcasted_iota(jnp.int32, (t, half), 0).astype(F32)
    fwd = lax.broadcasted_iota(jnp.int32, (t, half), 1) < SSM_STATE
    lw_re, lw_im = _cexp(jnp.where(fwd, (t - 1) - step, step), x, th)
    lw_re, lw_im = _cmul(lw_re, lw_im, q_re, q_im)
    lv_re, lv_im = _cexp(jnp.where(fwd, step + 1.0, t - step), x, th)
    a_re, a_im = _cexp(float(t), x, th)

    w_re, w_im = _cmul(_repeat_rows(lw_re), _repeat_rows(lw_im), _tile_rows(b_re), _tile_rows(b_im))
    w = jnp.concatenate([w_re, w_im], axis=1)
    v_re, v_im = _cmul(_repeat_rows(lv_re), _repeat_rows(lv_im), _tile_rows(c_re), _tile_rows(c_im))
    vt = jnp.concatenate([v_re, -v_im], axis=1)

    zero = jnp.zeros_like(c_re)
    fwd_c = lax.broadcasted_iota(jnp.int32, (SSM_GROUP, half), 1) < SSM_STATE
    c_stack = jnp.concatenate(
        [jnp.concatenate([jnp.where(fwd_c, c_re, zero), jnp.where(fwd_c, -c_im, zero)], axis=1),
         jnp.concatenate([jnp.where(fwd_c, zero, c_re), jnp.where(fwd_c, zero, -c_im)], axis=1)],
        axis=0)
    kk = lax.dot_general(c_stack, w, NT_DIMS, precision=lax.Precision.HIGHEST,
                         preferred_element_type=F32)
    pad = jnp.zeros((SSM_GROUP, CHUNK_K), F32)
    lane = lax.broadcasted_iota(jnp.int32, (SSM_GROUP, 2 * CHUNK_K), 1)
    hrow = lax.broadcasted_iota(jnp.int32, (SSM_GROUP, 2 * CHUNK_K), 0)
    center = (t - 1) * SSM_GROUP
    d_wide = jnp.concatenate(
        [jnp.zeros((1, center), F32), d_row, jnp.zeros((1, 2 * CHUNK_K - center - SSM_GROUP), F32)],
        axis=1)
    kall = (jnp.concatenate([kk[:SSM_GROUP], pad], axis=1)
            + jnp.concatenate([pad[:, :center], kk[SSM_GROUP:], pad[:, :SSM_GROUP]], axis=1)
            + jnp.where(lane - center == hrow, d_wide, 0.0))
    mt = jnp.concatenate(
        [kall[:, (t - 1 - to) * SSM_GROUP:(t - 1 - to) * SSM_GROUP + CHUNK_K] for to in range(t)],
        axis=0)
    return w, vt, mt, (a_re, a_im)


def _s5_kernel(n_chunks, n_batch, u_ref,
               lam_re_f, lam_im_f, log_step_f, b_re_f, b_im_f, c_re_f, c_im_f,
               lam_re_b, lam_im_b, log_step_b, b_re_b, b_im_b, c_re_b, c_im_b, d_ref,
               y_ref, s_scr, p_scr, w_scr, vt_scr, mt_scr):
    gb = u_ref.shape[0]
    half = 2 * SSM_STATE
    blk = SUBLANES
    n_blk = n_chunks // blk
    eye = (lax.broadcasted_iota(jnp.int32, (SSM_GROUP, SSM_GROUP), 0)
           == lax.broadcasted_iota(jnp.int32, (SSM_GROUP, SSM_GROUP), 1)).astype(F32)
    both = lambda f, b: jnp.concatenate([f, b], axis=1)
    as_hp = lambda f, b: lax.dot_general(eye, jnp.concatenate([f, b], axis=0), NT_DIMS,
                                         precision=lax.Precision.HIGHEST,
                                         preferred_element_type=F32)

    row = lax.broadcasted_iota(jnp.int32, (blk, half), 0)
    is_fwd = lax.broadcasted_iota(jnp.int32, (blk, half), 1) < SSM_STATE
    is_fwd_row = lax.broadcasted_iota(jnp.int32, (1, half), 1) < SSM_STATE
    by_dir = lambda f, b: (is_fwd & f) | (~is_fwd & b)
    edge = by_dir(row == 0, row == blk - 1)
    down = lambda v, s: jnp.where(is_fwd, pltpu.roll(v, s, 0), pltpu.roll(v, blk - s, 0))

    for j in range(gb):
        one = lambda ref: ref[j:j + 1, :]
        dt = both(jnp.broadcast_to(jnp.exp(one(log_step_f)), (1, SSM_STATE)),
                  jnp.broadcast_to(jnp.exp(one(log_step_b)), (1, SSM_STATE)))
        w, vt, mt, lam_t = _s5_group_operators(
            both(one(lam_re_f), one(lam_re_b)), both(one(lam_im_f), one(lam_im_b)), dt,
            as_hp(b_re_f[j], b_re_b[j]), as_hp(b_im_f[j], b_im_b[j]),
            both(c_re_f[j], c_re_b[j]), both(c_im_f[j], c_im_b[j]), one(d_ref))
        w_scr[...] = w.astype(BF16)
        vt_scr[j] = vt.astype(BF16)
        mt_scr[j] = mt.astype(BF16)
        s_scr[j] = jnp.dot(u_ref[j], w_scr[...], preferred_element_type=F32)

        pows = [lam_t]
        for _ in range(blk - 1):
            pows.append(_cmul(*pows[-1], *pows[0]))
        zero = jnp.zeros((blk, half), F32)
        pw_re, pw_im = zero, zero
        for r in range(blk):
            pw_re = jnp.where(row == r, jnp.where(is_fwd_row, pows[r][0], pows[blk - 1 - r][0]), pw_re)
            pw_im = jnp.where(row == r, jnp.where(is_fwd_row, pows[r][1], pows[blk - 1 - r][1]), pw_im)
        levels = []
        s = 1
        while s < blk:
            valid = by_dir(row >= s, row < blk - s)
            levels.append((s, jnp.where(valid, pows[s - 1][0], 0.0), jnp.where(valid, pows[s - 1][1], 0.0)))
            s *= 2

        def step(i, carry):
            new = []
            for b in range(n_batch):
                cr, ci = carry[b]
                rf = pl.multiple_of(b * n_chunks + i * blk, blk)
                rb = pl.multiple_of(b * n_chunks + (n_blk - 1 - i) * blk, blk)
                xr = jnp.where(is_fwd, s_scr[j, pl.ds(rf, blk), 0:half],
                               s_scr[j, pl.ds(rb, blk), 0:half])
                xi = jnp.where(is_fwd, s_scr[j, pl.ds(rf, blk), half:2 * half],
                               s_scr[j, pl.ds(rb, blk), half:2 * half])
                for s, m_re, m_im in levels:
                    dr, di = _cmul(m_re, m_im, down(xr, s), down(xi, s))
                    xr, xi = xr + dr, xi + di
                dr, di = _cmul(pw_re, pw_im, cr, ci)
                xr, xi = xr + dr, xi + di
                pr = jnp.where(edge, cr, down(xr, 1))
                pi = jnp.where(edge, ci, down(xi, 1))
                pltpu.store(p_scr.at[j, pl.ds(rf, blk), pl.ds(0, half)], pr, mask=is_fwd)
                pltpu.store(p_scr.at[j, pl.ds(rf, blk), pl.ds(half, half)], pi, mask=is_fwd)
                pltpu.store(p_scr.at[j, pl.ds(rb, blk), pl.ds(0, half)], pr, mask=~is_fwd)
                pltpu.store(p_scr.at[j, pl.ds(rb, blk), pl.ds(half, half)], pi, mask=~is_fwd)
                bcast = lambda v: jnp.where(is_fwd, jnp.broadcast_to(v[blk - 1:blk], (blk, half)),
                                            jnp.broadcast_to(v[0:1], (blk, half)))
                new.append((bcast(xr), bcast(xi)))
            return tuple(new)

        lax.fori_loop(0, n_blk, step, tuple((zero, zero) for _ in range(n_batch)))

    for j in range(gb):
        y = (lax.dot_general(u_ref[j], mt_scr[j], NT_DIMS, preferred_element_type=F32)
             + lax.dot_general(p_scr[j].astype(BF16), vt_scr[j], NT_DIMS,
                               preferred_element_type=F32))
        y_ref[j] = y.astype(y_ref.dtype)


def _s5(u, fwd, bwd, ssm_d, n_chunks, n_batch):
    groups, rows, _ = u.shape
    gb = S5_GROUP_BLOCK
    assert n_chunks % SUBLANES == 0 and rows == n_chunks * n_batch
    blk3 = lambda a, b: pl.BlockSpec((gb, a, b), lambda i: (i, 0, 0))
    blk2 = lambda a: pl.BlockSpec((gb, a), lambda i: (i, 0))
    direction = [blk2(SSM_STATE), blk2(SSM_STATE), blk2(1),
                 blk3(SSM_STATE, SSM_GROUP), blk3(SSM_STATE, SSM_GROUP),
                 blk3(SSM_GROUP, SSM_STATE), blk3(SSM_GROUP, SSM_STATE)]
    prep = lambda d: (d[0], d[1], d[2].reshape(groups, 1)) + tuple(d[3:])
    return pl.pallas_call(
        functools.partial(_s5_kernel, n_chunks, n_batch),
        grid=(groups // gb,),
        in_specs=[blk3(rows, CHUNK_K)] + direction + direction + [blk2(SSM_GROUP)],
        out_specs=blk3(rows, CHUNK_K),
        out_shape=jax.ShapeDtypeStruct((groups, rows, CHUNK_K), BF16),
        scratch_shapes=[pltpu.VMEM((gb, rows, STATE_W), F32),
                        pltpu.VMEM((gb, rows, STATE_W), F32),
                        pltpu.VMEM((CHUNK_K, STATE_W), BF16),
                        pltpu.VMEM((gb, CHUNK_K, STATE_W), BF16),
                        pltpu.VMEM((gb, CHUNK_K, CHUNK_K), BF16)],
        compiler_params=pltpu.CompilerParams(
            dimension_semantics=("arbitrary",), vmem_limit_bytes=VMEM_LIMIT_BYTES),
        name="s5_chunked",
    )(u, *prep(fwd), *prep(bwd), ssm_d.reshape(groups, SSM_GROUP))


def _head_rms_norm(x, head_avg, g):
    ms = jnp.dot((x * x).astype(BF16), head_avg, preferred_element_type=F32)
    return x * lax.rsqrt(ms + EPS) * g


def _mixer_ffn_kernel(tiles_per_seq,
                      h_ref, gprev_ref, g_ref, gnext_ref, y_ref,
                      conv_w_ref, conv_b_ref, ln_g_ref, ln_b_ref, conv_out_g_ref,
                      glu_w_ref, glu_b_ref, ssm_out_g_ref, head_avg_ref,
                      w_out_ref, b_out_ref, mix_post_g_ref,
                      pre_g_ref, wg_ref, wu_ref, wd_ref, post_g_ref,
                      out_ref, gbuf, dbuf):
    pos = pl.program_id(0) % tiles_per_seq
    gbuf[0:HALO, :] = jnp.where(pos == 0, 0.0, gprev_ref[...])
    gbuf[HALO:HALO + ROW_TILE, :] = g_ref[...]
    gbuf[HALO + ROW_TILE:, :] = jnp.where(pos == tiles_per_seq - 1, 0.0, gnext_ref[...])

    for r in range(0, ROW_TILE, CONV_ROW_BLOCK):
        acc = jnp.broadcast_to(conv_b_ref[...], (CONV_ROW_BLOCK, D_CONV))
        for k in range(CONV_WIDTH):
            start = r + HALO - CONV_PAD + k
            acc = acc + conv_w_ref[k:k + 1, :] * gbuf[start:start + CONV_ROW_BLOCK, :]
        dbuf[r:r + CONV_ROW_BLOCK, :] = acc

    d = dbuf[...]
    mu = jnp.mean(d, axis=-1, keepdims=True)
    dc = d - mu
    var = jnp.mean(dc * dc, axis=-1, keepdims=True)
    d = dc * lax.rsqrt(var + EPS) * ln_g_ref[...] + ln_b_ref[...]
    d = d * jax.nn.sigmoid(d)
    head_avg = head_avg_ref[...]
    conv_y = _head_rms_norm(d, head_avg, conv_out_g_ref[...])

    y = jax.nn.gelu(y_ref[...])
    gate = jnp.dot(y.astype(BF16), glu_w_ref[...], preferred_element_type=F32) + glu_b_ref[...]
    y = y * jax.nn.sigmoid(gate)
    ssm_y = _head_rms_norm(y, head_avg, ssm_out_g_ref[...])

    cat = jnp.concatenate([conv_y, ssm_y], axis=-1).astype(BF16)
    m = jnp.dot(cat, w_out_ref[...], preferred_element_type=F32) + b_out_ref[...]
    h = h_ref[...] + _rms_norm(m, mix_post_g_ref[...])
    out_ref[...] = _swiglu_half_step(h, pre_g_ref[...], wg_ref[...], wu_ref[...],
                                     wd_ref[...], post_g_ref[...])


def _mixer_ffn(h1, g, y, seq_len, conv_w, conv_b, ln_g, ln_b, conv_out_g,
               glu_w, glu_b, ssm_out_g, head_avg, w_out, b_out, mix_post_g,
               pre_g, wg, wu, wd, post_g):
    n = h1.shape[0]
    tiles_per_seq = seq_len // ROW_TILE
    halo_per_tile = ROW_TILE // HALO
    n_halo = n // HALO
    prev_spec = pl.BlockSpec(
        (HALO, D_CONV), lambda i: (jnp.maximum(i * halo_per_tile - 1, 0), 0))
    next_spec = pl.BlockSpec(
        (HALO, D_CONV), lambda i: (jnp.minimum((i + 1) * halo_per_tile, n_halo - 1), 0))
    return pl.pallas_call(
        functools.partial(_mixer_ffn_kernel, tiles_per_seq),
        grid=(n // ROW_TILE,),
        in_specs=[_rows(D_MODEL), prev_spec, _rows(D_CONV), next_spec, _rows(D_SSM),
                  _resident((CONV_WIDTH, D_CONV)), _resident((1, D_CONV)),
                  _resident((1, D_CONV)), _resident((1, D_CONV)), _resident((1, D_CONV)),
                  _resident((D_SSM, D_SSM)), _resident((1, D_SSM)), _resident((1, D_SSM)),
                  _resident((D_CONV, D_CONV)),
                  _resident((D_MODEL, D_MODEL)), _resident((1, D_MODEL)),
                  _resident((1, D_MODEL)),
                  _resident((1, D_MODEL)), _resident((D_MODEL, D_FF)),
                  _resident((D_MODEL, D_FF)), _resident((D_FF, D_MODEL)),
                  _resident((1, D_MODEL))],
        out_specs=_rows(D_MODEL),
        out_shape=jax.ShapeDtypeStruct((n, D_MODEL), F32),
        scratch_shapes=[pltpu.VMEM((ROW_TILE + 2 * HALO, D_CONV), F32),
                        pltpu.VMEM((ROW_TILE, D_CONV), F32)],
        compiler_params=pltpu.CompilerParams(
            dimension_semantics=("arbitrary",), vmem_limit_bytes=VMEM_LIMIT_BYTES),
        name="mixer_ffn2",
    )(h1, g, g, g, y, conv_w, conv_b, ln_g, ln_b, conv_out_g, glu_w, glu_b, ssm_out_g,
      head_avg, w_out, b_out, mix_post_g, pre_g, wg, wu, wd, post_g)


def kernel(x, ffn1_pre_g, ffn1_w_gate, ffn1_w_up, ffn1_w_down, ffn1_post_g, mix_pre_g, w_in, b_in, conv_w, conv_b, conv_ln_g, conv_ln_b, conv_out_g, lam_re_f, lam_im_f, log_step_f, b_re_f, b_im_f, c_re_f, c_im_f, lam_re_b, lam_im_b, log_step_b, b_re_b, b_im_b, c_re_b, c_im_b, ssm_d, ssm_glu_w, ssm_glu_b, ssm_out_g, w_out, b_out, mix_post_g, ffn2_pre_g, ffn2_w_gate, ffn2_w_up, ffn2_w_down, ffn2_post_g):
    bsz, seq, _ = x.shape
    n = bsz * seq
    n_chunks = seq // T_CHUNK
    row = lambda v: v.reshape(1, -1)
    h = x.reshape(n, D_MODEL)
    perm = jnp.asarray(_piece_transpose_matrix(), BF16)
    head_avg = jnp.asarray(np.kron(np.eye(D_CONV // OUT_HEAD_DIM),
                                   np.full((OUT_HEAD_DIM, OUT_HEAD_DIM), 1.0 / OUT_HEAD_DIM)), BF16)
    for l in range(ffn1_pre_g.shape[0]):
        h1, g, s = _ffn_in_proj(
            h, row(ffn1_pre_g[l]), ffn1_w_gate[l].astype(BF16), ffn1_w_up[l].astype(BF16),
            ffn1_w_down[l].astype(BF16), row(ffn1_post_g[l]), row(mix_pre_g[l]),
            w_in[l].astype(BF16), row(b_in[l]))

        u = _to_group_major(s, perm)
        yk = _s5(
            u,
            (lam_re_f[l], lam_im_f[l], log_step_f[l], b_re_f[l], b_im_f[l], c_re_f[l], c_im_f[l]),
            (lam_re_b[l], lam_im_b[l], log_step_b[l], b_re_b[l], b_im_b[l], c_re_b[l], c_im_b[l]),
            ssm_d[l], n_chunks, bsz)
        y = _from_group_major(yk, perm)

        h = _mixer_ffn(
            h1, g, y, seq, conv_w[l], row(conv_b[l]), row(conv_ln_g[l]), row(conv_ln_b[l]),
            row(conv_out_g[l]), ssm_glu_w[l].astype(BF16), row(ssm_glu_b[l]),
            row(ssm_out_g[l]), head_avg, w_out[l].astype(BF16), row(b_out[l]),
            row(mix_post_g[l]), row(ffn2_pre_g[l]), ffn2_w_gate[l].astype(BF16),
            ffn2_w_up[l].astype(BF16), ffn2_w_down[l].astype(BF16), row(ffn2_post_g[l]))
    return h.reshape(bsz, seq, D_MODEL)
```
